```python
import math
import jax, jax.numpy as jnp
from jax import lax
import numpy as np

D_MODEL = 2048
BATCH = 4
SEQ = 2048
DEPTH = 4

D_MIX = D_MODEL
ATT_WIDTH = D_MIX // 2
CONV_WIDTH = D_MIX // 4
RNN_WIDTH = D_MIX // 4
ATT_HEADS = 4
ATT_HEAD_DIM = ATT_WIDTH // (2 * ATT_HEADS)
ROPE_THETA = 10000.0
Q_BLOCK = 128
CONV_KERNEL = 31
RNN_CONV = 4
RNN_BLOCKS = 4
RNN_BLOCK_DIM = RNN_WIDTH // RNN_BLOCKS
RG_C = 8.0

Q_OFF = 0
K_OFF = Q_OFF + ATT_WIDTH
V_OFF = K_OFF + ATT_WIDTH
CONV_OFF = V_OFF + ATT_WIDTH
RNN_X_OFF = CONV_OFF + 2 * CONV_WIDTH
RNN_G_OFF = RNN_X_OFF + RNN_WIDTH
D_IN = RNN_G_OFF + RNN_WIDTH

N_EXPERTS = 16
N_GROUPS = 4
EXPERTS_PER_GROUP = N_EXPERTS // N_GROUPS
TOP_K = 2
D_EXPERT = D_MODEL // 2

ALPHA = (2 * DEPTH) ** 0.25
BETA = (8 * DEPTH) ** -0.25
EPS = 1e-5

kernel_name = "hybrid_diffattn_conformer_rglru_moe"


def layer_norm(x, g, b):
    xf = x.astype(jnp.float32)
    mu = jnp.mean(xf, axis=-1, keepdims=True)
    var = jnp.mean(jnp.square(xf - mu), axis=-1, keepdims=True)
    y = (xf - mu) * lax.rsqrt(var + EPS) * g.astype(jnp.float32) + b.astype(jnp.float32)
    return y.astype(x.dtype)


def rms_norm(x, g):
    xf = x.astype(jnp.float32)
    y = xf * lax.rsqrt(jnp.mean(jnp.square(xf), axis=-1, keepdims=True) + EPS) * g.astype(jnp.float32)
    return y.astype(x.dtype)


def rope_tables(seq_len):
    pos = jnp.arange(seq_len, dtype=jnp.float32)
    inv_freq = ROPE_THETA ** (-jnp.arange(0, ATT_HEAD_DIM, 2, dtype=jnp.float32) / ATT_HEAD_DIM)
    ang = pos[:, None] * inv_freq[None, :]
    return jnp.cos(ang), jnp.sin(ang)


def apply_rope(t, cos, sin):
    c = cos[None, :, None, None, :]
    s = sin[None, :, None, None, :]
    tf = t.astype(jnp.float32)
    t1, t2 = jnp.split(tf, 2, axis=-1)
    out = jnp.concatenate([t1 * c - t2 * s, t2 * c + t1 * s], axis=-1)
    return out.astype(t.dtype)


def diff_attention(q, k, v, lam):
    B, S = q.shape[0], q.shape[1]
    n_blocks = S // Q_BLOCK
    scale = ATT_HEAD_DIM ** -0.5
    k_pos = jnp.arange(S)

    def block(i):
        start = i * Q_BLOCK
        qb = lax.dynamic_slice_in_dim(q, start, Q_BLOCK, axis=1)
        s = jnp.einsum('bqhcd,bkhcd->bhcqk', qb, k).astype(jnp.float32) * scale
        q_pos = start + jnp.arange(Q_BLOCK)
        mask = k_pos[None, :] <= q_pos[:, None]
        p = jax.nn.softmax(jnp.where(mask, s, -jnp.inf), axis=-1)
        a = p[:, :, 0] - lam * p[:, :, 1]
        return jnp.einsum('bhqk,bkhe->bqhe', a.astype(v.dtype), v)

    o = lax.map(block, jnp.arange(n_blocks))
    return jnp.moveaxis(o, 0, 1).reshape(B, S, ATT_HEADS, 2 * ATT_HEAD_DIM)


def causal_depthwise_conv(x, w):
    kw, c = w.shape
    return lax.conv_general_dilated(
        x, w[:, None, :].astype(x.dtype), window_strides=(1,), padding=[(kw - 1, 0)],
        dimension_numbers=('NWC', 'WIO', 'NWC'), feature_group_count=c)


def conformer_conv(u, dw, ln_g, ln_b, pw):
    g = u[..., :CONV_WIDTH] * jax.nn.sigmoid(u[..., CONV_WIDTH:])
    g = causal_depthwise_conv(g, dw)
    g = jax.nn.silu(layer_norm(g, ln_g, ln_b))
    return g @ pw


def _linear_combine(c1, c2):
    a1, b1 = c1
    a2, b2 = c2
    return a1 * a2, a2 * b1 + b2


def rglru_block(xr, gate, conv_w, conv_b, wa, ba, wx, bx, lam):
    B, S, _ = xr.shape
    xc = causal_depthwise_conv(xr, conv_w) + conv_b
    xb = xc.reshape(B, S, RNN_BLOCKS, RNN_BLOCK_DIM)
    r = jax.nn.sigmoid(jnp.einsum('bsni,nij->bsnj', xb, wa).reshape(B, S, RNN_WIDTH) + ba)
    i = jax.nn.sigmoid(jnp.einsum('bsni,nij->bsnj', xb, wx).reshape(B, S, RNN_WIDTH) + bx)
    log_a = -RG_C * r.astype(jnp.float32) * jax.nn.softplus(-lam.astype(jnp.float32))
    a = jnp.exp(log_a)
    b = jnp.sqrt(-jnp.expm1(2.0 * log_a)) * (i * xc).astype(jnp.float32)
    _, h = lax.associative_scan(_linear_combine, (a, b), axis=1)
    return h.astype(xr.dtype) * jax.nn.gelu(gate)


def grouped_moe(x, router_w, router_b, wg, wu, wd):
    B, S, D = x.shape
    xt = x.reshape(B * S, D)
    scores = jax.nn.sigmoid((xt @ router_w).astype(jnp.float32))
    biased = scores + router_b.astype(jnp.float32)
    grp = biased.reshape(-1, N_GROUPS, EXPERTS_PER_GROUP)
    group_score = jnp.sum(lax.top_k(grp, 2)[0], axis=-1)
    sel_group = jnp.argmax(group_score, axis=-1)
    in_group = (jnp.arange(N_EXPERTS) // EXPERTS_PER_GROUP)[None, :] == sel_group[:, None]
    _, idx = lax.top_k(jnp.where(in_group, biased, -jnp.inf), TOP_K)
    w_sel = jnp.take_along_axis(scores, idx, axis=1)
    w_sel = w_sel / jnp.sum(w_sel, axis=-1, keepdims=True)
    combine = jnp.einsum('nk,nke->ne', w_sel, jax.nn.one_hot(idx, N_EXPERTS, dtype=jnp.float32))
    combine = combine.astype(x.dtype)
    y = jnp.zeros_like(xt)
    for e in range(N_EXPERTS):
        h = jax.nn.silu(xt @ wg[e]) * (xt @ wu[e])
        y = y + combine[:, e:e + 1] * (h @ wd[e])
    return y.reshape(B, S, D)


def hybrid_mixer(h, layer, w_in, lq1, lk1, lq2, lk2, subln_g, conv_dw, conv_ln_g, conv_ln_b,
                 conv_pw, rg_conv_w, rg_conv_b, rg_wa, rg_ba, rg_wx, rg_bx, rg_lambda, w_out):
    B, S, _ = h.shape
    z = h @ w_in
    q = z[..., Q_OFF:K_OFF].reshape(B, S, ATT_HEADS, 2, ATT_HEAD_DIM)
    k = z[..., K_OFF:V_OFF].reshape(B, S, ATT_HEADS, 2, ATT_HEAD_DIM)
    v = z[..., V_OFF:CONV_OFF].reshape(B, S, ATT_HEADS, 2 * ATT_HEAD_DIM)
    cos, sin = rope_tables(S)
    q = apply_rope(q, cos, sin)
    k = apply_rope(k, cos, sin)
    lam_init = 0.8 - 0.6 * math.exp(-0.3 * layer)
    lam = (jnp.exp(jnp.sum(lq1.astype(jnp.float32) * lk1.astype(jnp.float32)))
           - jnp.exp(jnp.sum(lq2.astype(jnp.float32) * lk2.astype(jnp.float32))) + lam_init)
    o = diff_attention(q, k, v, lam)
    attn_out = (rms_norm(o, subln_g) * (1.0 - lam_init)).reshape(B, S, ATT_WIDTH)
    conv_out = conformer_conv(z[..., CONV_OFF:RNN_X_OFF], conv_dw, conv_ln_g, conv_ln_b, conv_pw)
    rnn_out = rglru_block(z[..., RNN_X_OFF:RNN_G_OFF], z[..., RNN_G_OFF:D_IN], rg_conv_w, rg_conv_b,
                          rg_wa, rg_ba, rg_wx, rg_bx, rg_lambda)
    cat = jnp.concatenate([attn_out, conv_out, rnn_out], axis=-1)
    return cat @ w_out


def setup_inputs(seed: int = 0) -> dict:
    key = jax.random.key(seed)
    ks = jax.random.split(key, 32)
    f32 = jnp.float32
    nrm = lambda k, shape, s: jax.random.normal(k, shape, f32) * s
    u = jax.random.uniform(ks[17], (DEPTH, RNN_WIDTH), f32, minval=0.9, maxval=0.999)
    a0 = u ** (1.0 / RG_C)
    return {
        "x": nrm(ks[0], (BATCH, SEQ, D_MODEL), 1.0),
        "w_in": nrm(ks[1], (DEPTH, D_MODEL, D_IN), D_MODEL ** -0.5),
        "lam_q1": nrm(ks[2], (DEPTH, ATT_HEAD_DIM), 0.1),
        "lam_k1": nrm(ks[3], (DEPTH, ATT_HEAD_DIM), 0.1),
        "lam_q2": nrm(ks[4], (DEPTH, ATT_HEAD_DIM), 0.1),
        "lam_k2": nrm(ks[5], (DEPTH, ATT_HEAD_DIM), 0.1),
        "subln_g": 1.0 + nrm(ks[6], (DEPTH, 2 * ATT_HEAD_DIM), 0.02),
        "conv_dw": nrm(ks[7], (DEPTH, CONV_KERNEL, CONV_WIDTH), CONV_KERNEL ** -0.5),
        "conv_ln_g": 1.0 + nrm(ks[8], (DEPTH, CONV_WIDTH), 0.02),
        "conv_ln_b": nrm(ks[9], (DEPTH, CONV_WIDTH), 0.02),
        "conv_pw": nrm(ks[10], (DEPTH, CONV_WIDTH, CONV_WIDTH), CONV_WIDTH ** -0.5),
        "rg_conv_w": nrm(ks[11], (DEPTH, RNN_CONV, RNN_WIDTH), RNN_CONV ** -0.5),
        "rg_conv_b": nrm(ks[12], (DEPTH, RNN_WIDTH), 0.02),
        "rg_wa": nrm(ks[13], (DEPTH, RNN_BLOCKS, RNN_BLOCK_DIM, RNN_BLOCK_DIM), RNN_BLOCK_DIM ** -0.5),
        "rg_ba": nrm(ks[14], (DEPTH, RNN_WIDTH), 0.02),
        "rg_wx": nrm(ks[15], (DEPTH, RNN_BLOCKS, RNN_BLOCK_DIM, RNN_BLOCK_DIM), RNN_BLOCK_DIM ** -0.5),
        "rg_bx": nrm(ks[16], (DEPTH, RNN_WIDTH), 0.02),
        "rg_lambda": jnp.log(a0) - jnp.log1p(-a0),
        "w_out": nrm(ks[18], (DEPTH, D_MIX, D_MODEL), D_MIX ** -0.5 * BETA),
        "ln1_g": 1.0 + nrm(ks[19], (DEPTH, D_MODEL), 0.02),
        "ln1_b": nrm(ks[20], (DEPTH, D_MODEL), 0.02),
        "router_w": nrm(ks[21], (D_MODEL, N_EXPERTS), D_MODEL ** -0.5),
        "router_b": nrm(ks[22], (N_EXPERTS,), 0.01),
        "exp_wg": nrm(ks[23], (DEPTH, N_EXPERTS, D_MODEL, D_EXPERT), D_MODEL ** -0.5),
        "exp_wu": nrm(ks[24], (DEPTH, N_EXPERTS, D_MODEL, D_EXPERT), D_MODEL ** -0.5),
        "exp_wd": nrm(ks[25], (DEPTH, N_EXPERTS, D_EXPERT, D_MODEL), D_EXPERT ** -0.5 * BETA),
        "ln2_g": 1.0 + nrm(ks[26], (DEPTH, D_MODEL), 0.02),
        "ln2_b": nrm(ks[27], (DEPTH, D_MODEL), 0.02),
    }


def reference(x, w_in, lam_q1, lam_k1, lam_q2, lam_k2, subln_g, conv_dw, conv_ln_g, conv_ln_b,
              conv_pw, rg_conv_w, rg_conv_b, rg_wa, rg_ba, rg_wx, rg_bx, rg_lambda, w_out,
              ln1_g, ln1_b, router_w, router_b, exp_wg, exp_wu, exp_wd, ln2_g, ln2_b):
    for l in range(DEPTH):
        mix = hybrid_mixer(x, l, w_in[l], lam_q1[l], lam_k1[l], lam_q2[l], lam_k2[l], subln_g[l],
                           conv_dw[l], conv_ln_g[l], conv_ln_b[l], conv_pw[l], rg_conv_w[l],
                           rg_conv_b[l], rg_wa[l], rg_ba[l], rg_wx[l], rg_bx[l], rg_lambda[l], w_out[l])
        x = layer_norm(ALPHA * x + mix, ln1_g[l], ln1_b[l])
        ffn = grouped_moe(x, router_w, router_b, exp_wg[l], exp_wu[l], exp_wd[l])
        x = layer_norm(ALPHA * x + ffn, ln2_g[l], ln2_b[l])
    return x
```

```python
import functools
import math

import jax
import jax.numpy as jnp
from jax import lax
from jax.experimental import pallas as pl
from jax.experimental.pallas import tpu as pltpu

F32 = jnp.float32
BF16 = jnp.bfloat16

ATT_HEADS = 4
HEAD_DIM = 128
ROPE_THETA = 10000.0
CONV_KERNEL = 31
RNN_CONV = 4
RNN_BLOCKS = 4
RG_C = 8.0
N_EXPERTS = 16
N_GROUPS = 4
EXPERTS_PER_GROUP = N_EXPERTS // N_GROUPS
EPS = 1e-5

LANES = 128
SUBLANES = 8
VMEM_LIMIT = 52 * 1024 * 1024

PROJ_TM = 1024
PROJ_TN = 1024
ATT_TQ = 256
CONV_RC = 64
CONV_PAD = 32
RNN_RC = 256
RNN_PAD = 8
OUT_TM = 256
MOE_TM = 256
ROW_TD = 256


def _cparams(*sem):
    return pltpu.CompilerParams(dimension_semantics=sem, vmem_limit_bytes=VMEM_LIMIT)


def _layer_norm(z, g, b):
    mu = jnp.mean(z, axis=-1, keepdims=True)
    zc = z - mu
    var = jnp.mean(zc * zc, axis=-1, keepdims=True)
    return zc * lax.rsqrt(var + EPS) * g + b


def _row_windows(win, first, taps, rows):
    total = win.shape[0]
    rolled = {}
    out = []
    for j in range(taps):
        shift = (first + j) % SUBLANES
        if shift not in rolled:
            rolled[shift] = pltpu.roll(win, total - shift, axis=0) if shift else win
        base = first + j - shift
        out.append(rolled[shift][base:base + rows])
    return out


def _inproj_qkv_kernel(x_ref, w_ref, cos_ref, sin_ref, o_ref, *, scale):
    j = pl.program_id(0)
    acc = jnp.dot(x_ref[...], w_ref[...], preferred_element_type=F32)

    @pl.when(j < 2)
    def _():
        mult = jnp.where(j == 0, scale, 1.0).astype(F32)
        cos = cos_ref[...] * mult
        sin = sin_ref[...] * mult
        for c in range(PROJ_TN // LANES):
            t = acc[:, c * LANES:(c + 1) * LANES]
            r = t * cos + pltpu.roll(t, HEAD_DIM // 2, axis=1) * sin
            o_ref[:, c * LANES:(c + 1) * LANES] = r.astype(o_ref.dtype)

    @pl.when(j >= 2)
    def _():
        o_ref[...] = acc.astype(o_ref.dtype)


def _inproj_plain_kernel(x_ref, w_ref, o_ref):
    o_ref[...] = jnp.dot(x_ref[...], w_ref[...], preferred_element_type=F32).astype(o_ref.dtype)


def _inproj_qkv(xb, w, cos2, sin2, seq):
    n, d = xb.shape
    width = w.shape[1]
    pos_blocks = seq // PROJ_TM
    return pl.pallas_call(
        functools.partial(_inproj_qkv_kernel, scale=HEAD_DIM ** -0.5),
        grid=(width // PROJ_TN, n // PROJ_TM),
        in_specs=[
            pl.BlockSpec((PROJ_TM, d), lambda j, i: (i, 0)),
            pl.BlockSpec((d, PROJ_TN), lambda j, i: (0, j)),
            pl.BlockSpec((PROJ_TM, LANES), lambda j, i: (i % pos_blocks, 0)),
            pl.BlockSpec((PROJ_TM, LANES), lambda j, i: (i % pos_blocks, 0)),
        ],
        out_specs=pl.BlockSpec((PROJ_TM, PROJ_TN), lambda j, i: (i, j)),
        out_shape=jax.ShapeDtypeStruct((n, width), BF16),
        compiler_params=_cparams("arbitrary", "arbitrary"),
        name="inproj_qkv",
    )(xb, w, cos2, sin2)


def _inproj_plain(xb, w):
    n, d = xb.shape
    width = w.shape[1]
    return pl.pallas_call(
        _inproj_plain_kernel,
        grid=(width // PROJ_TN, n // PROJ_TM),
        in_specs=[
            pl.BlockSpec((PROJ_TM, d), lambda j, i: (i, 0)),
            pl.BlockSpec((d, PROJ_TN), lambda j, i: (0, j)),
        ],
        out_specs=pl.BlockSpec((PROJ_TM, PROJ_TN), lambda j, i: (i, j)),
        out_shape=jax.ShapeDtypeStruct((n, width), F32),
        compiler_params=_cparams("arbitrary", "arbitrary"),
        name="inproj_cr",
    )(xb, w)


def _attn_kernel(q_ref, k_ref, v_ref, lamv_ref, g_ref, o_ref, *, lam_init, seq):
    lamv = lamv_ref[...]
    lam = (jnp.exp(jnp.sum(lamv[0:1, :] * lamv[1:2, :], keepdims=True))
           - jnp.exp(jnp.sum(lamv[2:3, :] * lamv[3:4, :], keepdims=True)) + lam_init)
    kt1 = k_ref[:, 0:HEAD_DIM].T
    kt2 = k_ref[:, HEAD_DIM:2 * HEAD_DIM].T
    gain = g_ref[...] * (1.0 - lam_init)
    for qi in range(seq // ATT_TQ):
        lo, hi = qi * ATT_TQ, (qi + 1) * ATT_TQ
        row = lo + lax.broadcasted_iota(jnp.int32, (ATT_TQ, hi), 0)
        col = lax.broadcasted_iota(jnp.int32, (ATT_TQ, hi), 1)
        keep = col <= row

        def softmax_terms(q, kt):
            s = jnp.dot(q, kt[:, 0:hi], preferred_element_type=F32)
            s = jnp.where(keep, s, -jnp.inf)
            e = jnp.exp(s - jnp.max(s, axis=-1, keepdims=True))
            return e, jnp.sum(e, axis=-1, keepdims=True)

        e1, l1 = softmax_terms(q_ref[lo:hi, 0:HEAD_DIM], kt1)
        e2, l2 = softmax_terms(q_ref[lo:hi, HEAD_DIM:2 * HEAD_DIM], kt2)
        a = e1 * (1.0 / l1) - e2 * (lam / l2)
        o = jnp.dot(a.astype(BF16), v_ref[0:hi, :], preferred_element_type=F32)
        o = o * lax.rsqrt(jnp.mean(o * o, axis=-1, keepdims=True) + EPS) * gain
        o_ref[lo:hi, :] = o.astype(o_ref.dtype)


def _attention(qkv, lamv, subln_g, layer, batch, seq):
    vw = 2 * HEAD_DIM
    lam_init = 0.8 - 0.6 * math.exp(-0.3 * layer)
    return pl.pallas_call(
        functools.partial(_attn_kernel, lam_init=lam_init, seq=seq),
        grid=(batch, ATT_HEADS),
        in_specs=[
            pl.BlockSpec((seq, vw), lambda b, h: (b, h)),
            pl.BlockSpec((seq, vw), lambda b, h: (b, ATT_HEADS + h)),
            pl.BlockSpec((seq, vw), lambda b, h: (b, 2 * ATT_HEADS + h)),
            pl.BlockSpec((4, HEAD_DIM), lambda b, h: (0, 0)),
            pl.BlockSpec((1, vw), lambda b, h: (0, 0)),
        ],
        out_specs=pl.BlockSpec((seq, vw), lambda b, h: (b, h)),
        out_shape=jax.ShapeDtypeStruct((batch * seq, ATT_HEADS * vw), BF16),
        compiler_params=_cparams("arbitrary", "arbitrary"),
        name="diff_attention",
    )(qkv, qkv, qkv, lamv, subln_g)


def _conv_kernel(u_ref, dw_ref, g_ref, b_ref, pw_ref, o_ref, gpad_ref, y_ref, *, seq, width):
    gpad_ref[0:CONV_PAD, :] = jnp.zeros((CONV_PAD, width), F32)

    def glu_step(i, carry):
        r0 = pl.multiple_of(i * CONV_RC, CONV_RC)
        u = u_ref[pl.ds(r0, CONV_RC), :]
        gpad_ref[pl.ds(CONV_PAD + r0, CONV_RC), :] = u[:, 0:width] * jax.nn.sigmoid(u[:, width:2 * width])
        return carry

    lax.fori_loop(0, seq // CONV_RC, glu_step, 0)

    ln_g = g_ref[...]
    ln_b = b_ref[...]
    first = CONV_PAD - (CONV_KERNEL - 1)

    def conv_step(i, carry):
        r0 = pl.multiple_of(i * CONV_RC, CONV_RC)
        win = gpad_ref[pl.ds(r0, CONV_RC + CONV_PAD), :]
        acc = jnp.zeros((CONV_RC, width), F32)
        for j, rows in enumerate(_row_windows(win, first, CONV_KERNEL, CONV_RC)):
            acc = acc + dw_ref[j:j + 1, :] * rows
        y = _layer_norm(acc, ln_g, ln_b)
        y = y * jax.nn.sigmoid(y)
        y_ref[pl.ds(r0, CONV_RC), :] = y.astype(y_ref.dtype)
        return carry

    lax.fori_loop(0, seq // CONV_RC, conv_step, 0)
    o_ref[...] = jnp.dot(y_ref[...], pw_ref[...], preferred_element_type=F32).astype(o_ref.dtype)


def _conformer_conv(cr, dw, ln_g, ln_b, pw, batch, seq):
    width = pw.shape[0]
    return pl.pallas_call(
        functools.partial(_conv_kernel, seq=seq, width=width),
        grid=(batch,),
        in_specs=[
            pl.BlockSpec((seq, 2 * width), lambda b: (b, 0)),
            pl.BlockSpec((CONV_KERNEL, width), lambda b: (0, 0)),
            pl.BlockSpec((1, width), lambda b: (0, 0)),
            pl.BlockSpec((1, width), lambda b: (0, 0)),
            pl.BlockSpec((width, width), lambda b: (0, 0)),
        ],
        out_specs=pl.BlockSpec((seq, width), lambda b: (b, 0)),
        out_shape=jax.ShapeDtypeStruct((batch * seq, width), BF16),
        scratch_shapes=[pltpu.VMEM((CONV_PAD + seq, width), F32), pltpu.VMEM((seq, width), BF16)],
        compiler_params=_cparams("arbitrary"),
        name="conformer_conv",
    )(cr, dw, ln_g, ln_b, pw)


def _rglru_kernel(x_ref, gate_ref, cw_ref, cb_ref, wa_ref, ba_ref, wx_ref, bx_ref, lam_ref, o_ref,
                  xpad_ref, a_ref, b_ref, *, seq, width):
    bd = width // RNN_BLOCKS
    xpad_ref[0:RNN_PAD, :] = jnp.zeros((RNN_PAD, width), F32)
    xpad_ref[RNN_PAD:RNN_PAD + seq, :] = x_ref[...]

    lam = lam_ref[...]
    neg = -lam
    softplus = jnp.maximum(neg, 0.0) + jnp.log1p(jnp.exp(-jnp.abs(neg)))
    decay = -RG_C * softplus
    cb = cb_ref[...]
    ba = ba_ref[...]
    bx = bx_ref[...]
    first = RNN_PAD - (RNN_CONV - 1)

    def gate_step(i, carry):
        r0 = pl.multiple_of(i * RNN_RC, RNN_RC)
        win = xpad_ref[pl.ds(r0, RNN_RC + RNN_PAD), :]
        xc = jnp.zeros((RNN_RC, width), F32) + cb
        for j, rows in enumerate(_row_windows(win, first, RNN_CONV, RNN_RC)):
            xc = xc + cw_ref[j:j + 1, :] * rows
        xcb = xc.astype(BF16)
        ra = jnp.concatenate(
            [jnp.dot(xcb[:, n * bd:(n + 1) * bd], wa_ref[n], preferred_element_type=F32)
             for n in range(RNN_BLOCKS)], axis=1)
        rx = jnp.concatenate(
            [jnp.dot(xcb[:, n * bd:(n + 1) * bd], wx_ref[n], preferred_element_type=F32)
             for n in range(RNN_BLOCKS)], axis=1)
        r = jax.nn.sigmoid(ra + ba)
        ig = jax.nn.sigmoid(rx + bx)
        log_a = decay * r
        a = jnp.exp(log_a)
        a_ref[pl.ds(r0, RNN_RC), :] = a
        b_ref[pl.ds(r0, RNN_RC), :] = jnp.sqrt(-jnp.tanh(log_a) * (a * a + 1.0)) * (ig * xc)
        return carry

    lax.fori_loop(0, seq // RNN_RC, gate_step, 0)

    rows = lax.broadcasted_iota(jnp.int32, (SUBLANES, width), 0)

    def scan_step(c, h_prev):
        r0 = pl.multiple_of(c * SUBLANES, SUBLANES)
        a = a_ref[pl.ds(r0, SUBLANES), :]
        b = b_ref[pl.ds(r0, SUBLANES), :]
        for sh in (1, 2, 4):
            valid = rows >= sh
            b = jnp.where(valid, a * pltpu.roll(b, sh, axis=0) + b, b)
            a = jnp.where(valid, a * pltpu.roll(a, sh, axis=0), a)
        h = a * h_prev + b
        b_ref[pl.ds(r0, SUBLANES), :] = h
        return jnp.broadcast_to(h[SUBLANES - 1:SUBLANES, :], (SUBLANES, width))

    lax.fori_loop(0, seq // SUBLANES, scan_step, jnp.zeros((SUBLANES, width), F32))

    def out_step(i, carry):
        r0 = pl.multiple_of(i * RNN_RC, RNN_RC)
        g = gate_ref[pl.ds(r0, RNN_RC), :]
        cdf = 0.5 * (1.0 + jnp.tanh(math.sqrt(2.0 / math.pi) * (g + 0.044715 * (g * g * g))))
        o_ref[pl.ds(r0, RNN_RC), :] = (b_ref[pl.ds(r0, RNN_RC), :] * (g * cdf)).astype(o_ref.dtype)
        return carry

    lax.fori_loop(0, seq // RNN_RC, out_step, 0)


def _rglru(cr, cw, cb, wa, ba, wx, bx, lam, batch, seq, x_blk, gate_blk):
    width = cw.shape[1]
    bd = width // RNN_BLOCKS
    vec = pl.BlockSpec((1, width), lambda b: (0, 0))
    blk = pl.BlockSpec((RNN_BLOCKS, bd, bd), lambda b: (0, 0, 0))
    return pl.pallas_call(
        functools.partial(_rglru_kernel, seq=seq, width=width),
        grid=(batch,),
        in_specs=[
            pl.BlockSpec((seq, width), lambda b: (b, x_blk)),
            pl.BlockSpec((seq, width), lambda b: (b, gate_blk)),
            pl.BlockSpec((RNN_CONV, width), lambda b: (0, 0)),
            vec, blk, vec, blk, vec, vec,
        ],
        out_specs=pl.BlockSpec((seq, width), lambda b: (b, 0)),
        out_shape=jax.ShapeDtypeStruct((batch * seq, width), BF16),
        scratch_shapes=[pltpu.VMEM((RNN_PAD + seq, width), F32), pltpu.VMEM((seq, width), F32),
                        pltpu.VMEM((seq, width), F32)],
        compiler_params=_cparams("arbitrary"),
        name="rglru",
    )(cr, cr, cw, cb, wa, ba, wx, bx, lam)


def _route(scores, biased):
    neg_inf = jnp.float32(-jnp.inf)
    group_score = []
    for g in range(N_GROUPS):
        v = biased[g * EXPERTS_PER_GROUP:(g + 1) * EXPERTS_PER_GROUP]
        best = None
        for i in range(EXPERTS_PER_GROUP):
            for j in range(i + 1, EXPERTS_PER_GROUP):
                pair = v[i] + v[j]
                best = pair if best is None else jnp.maximum(best, pair)
        group_score.append(best)
    sel = jnp.zeros_like(group_score[0], dtype=jnp.int32)
    top = group_score[0]
    for g in range(1, N_GROUPS):
        better = group_score[g] > top
        sel = jnp.where(better, g, sel)
        top = jnp.where(better, group_score[g], top)
    masked = [jnp.where(sel == (e // EXPERTS_PER_GROUP), biased[e], neg_inf) for e in range(N_EXPERTS)]

    def first_argmax(vals):
        m = vals[0]
        for e in range(1, N_EXPERTS):
            m = jnp.maximum(m, vals[e])
        idx = jnp.full_like(sel, N_EXPERTS)
        for e in range(N_EXPERTS - 1, -1, -1):
            idx = jnp.where(vals[e] == m, e, idx)
        return idx

    idx1 = first_argmax(masked)
    idx2 = first_argmax([jnp.where(idx1 == e, neg_inf, masked[e]) for e in range(N_EXPERTS)])
    s1 = jnp.zeros_like(top)
    s2 = jnp.zeros_like(top)
    for e in range(N_EXPERTS):
        s1 = jnp.where(idx1 == e, scores[e], s1)
        s2 = jnp.where(idx2 == e, scores[e], s2)
    tot = s1 + s2
    return idx1, idx2, s1 / tot, s2 / tot


def _outproj_kernel(att_ref, cv_ref, rn_ref, w_ref, x_ref, g_ref, b_ref, rw_ref, rb_ref,
                    x1_ref, idx_ref, wt_ref, *, alpha):
    cat = jnp.concatenate([att_ref[...], cv_ref[...], rn_ref[...]], axis=1)
    mix = jnp.dot(cat, w_ref[...], preferred_element_type=F32)
    x1 = _layer_norm(alpha * x_ref[...] + mix, g_ref[...], b_ref[...])
    x1_ref[...] = x1
    logits = jnp.dot(x1.astype(BF16), rw_ref[...], preferred_element_type=F32)
    lt = logits.T
    sc = jax.nn.sigmoid(lt[0:N_EXPERTS, :])
    bs = sc + rb_ref[...]
    scores = [sc[e:e + 1, :] for e in range(N_EXPERTS)]
    biased = [bs[e:e + 1, :] for e in range(N_EXPERTS)]
    idx1, idx2, w1, w2 = _route(scores, biased)
    t = x1.shape[0]
    idx_ref[...] = jnp.concatenate([idx1, idx2, jnp.zeros((SUBLANES - 2, t), jnp.int32)], axis=0)
    wt_ref[...] = jnp.concatenate([w1, w2, jnp.zeros((LANES - 2, t), F32)], axis=0).T


def _outproj_route(att, cv, rn, w, x, g, b, rw_pad, rb_col, alpha):
    n, d = x.shape
    row = lambda width: pl.BlockSpec((OUT_TM, width), lambda i: (i, 0))
    full = lambda shape: pl.BlockSpec(shape, lambda i: (0,) * len(shape))
    return pl.pallas_call(
        functools.partial(_outproj_kernel, alpha=alpha),
        grid=(n // OUT_TM,),
        in_specs=[row(att.shape[1]), row(cv.shape[1]), row(rn.shape[1]), full(w.shape), row(d),
                  full((1, d)), full((1, d)), full(rw_pad.shape), full(rb_col.shape)],
        out_specs=[row(d), pl.BlockSpec((SUBLANES, OUT_TM), lambda i: (0, i)), row(LANES)],
        out_shape=[jax.ShapeDtypeStruct((n, d), F32), jax.ShapeDtypeStruct((SUBLANES, n), jnp.int32),
                   jax.ShapeDtypeStruct((n, LANES), F32)],
        compiler_params=_cparams("arbitrary"),
        name="outproj_ln_router",
    )(att, cv, rn, w, x, g, b, rw_pad, rb_col)


def _dispatch_kernel(pos_ref, x_ref, init_ref, xs_ref, sem, *, n):
    del init_ref
    i = pl.program_id(0)

    def row_copy(r, k):
        p = pos_ref[k * n + i * ROW_TD + r]
        return pltpu.make_async_copy(x_ref.at[pl.ds(r, 1)], xs_ref.at[pl.ds(p, 1)], sem)

    def issue(r, carry):
        row_copy(r, 0).start()
        row_copy(r, 1).start()
        return carry

    def drain(r, carry):
        row_copy(r, 0).wait()
        row_copy(r, 1).wait()
        return carry

    lax.fori_loop(0, ROW_TD, issue, 0)
    lax.fori_loop(0, ROW_TD, drain, 0)


def _dispatch(pos, x1, m_pad):
    n, d = x1.shape
    return pl.pallas_call(
        functools.partial(_dispatch_kernel, n=n),
        grid_spec=pltpu.PrefetchScalarGridSpec(
            num_scalar_prefetch=1, grid=(n // ROW_TD,),
            in_specs=[pl.BlockSpec((ROW_TD, d), lambda i, p: (i, 0)), pl.BlockSpec(memory_space=pl.ANY)],
            out_specs=pl.BlockSpec(memory_space=pl.ANY),
            scratch_shapes=[pltpu.SemaphoreType.DMA]),
        out_shape=jax.ShapeDtypeStruct((m_pad, d), F32),
        input_output_aliases={2: 0},
        compiler_params=_cparams("arbitrary"),
        name="moe_dispatch",
    )(pos, x1, jnp.zeros((m_pad, d), F32))


def _moe_kernel(te_ref, nu_ref, xs_ref, wg_ref, wu_ref, wd_ref, ys_ref):
    t = pl.program_id(0)

    @pl.when(t < nu_ref[0])
    def _():
        x = xs_ref[...].astype(BF16)
        g = jnp.dot(x, wg_ref[0], preferred_element_type=F32)
        u = jnp.dot(x, wu_ref[0], preferred_element_type=F32)
        h = (g * jax.nn.sigmoid(g)) * u
        ys_ref[...] = jnp.dot(h.astype(BF16), wd_ref[0], preferred_element_type=F32)

    @pl.when(t >= nu_ref[0])
    def _():
        ys_ref[...] = jnp.zeros_like(ys_ref)


def _moe(tile_e, n_used, xs, wg, wu, wd):
    m_pad, d = xs.shape
    de = wg.shape[2]
    return pl.pallas_call(
        _moe_kernel,
        grid_spec=pltpu.PrefetchScalarGridSpec(
            num_scalar_prefetch=2, grid=(m_pad // MOE_TM,),
            in_specs=[
                pl.BlockSpec((MOE_TM, d), lambda t, te, nu: (t, 0)),
                pl.BlockSpec((1, d, de), lambda t, te, nu: (te[t], 0, 0)),
                pl.BlockSpec((1, d, de), lambda t, te, nu: (te[t], 0, 0)),
                pl.BlockSpec((1, de, d), lambda t, te, nu: (te[t], 0, 0)),
            ],
            out_specs=pl.BlockSpec((MOE_TM, d), lambda t, te, nu: (t, 0))),
        out_shape=jax.ShapeDtypeStruct((m_pad, d), F32),
        compiler_params=_cparams("arbitrary"),
        name="moe_experts",
    )(tile_e, n_used, xs, wg, wu, wd)


def _combine_kernel(pos_ref, x_ref, wt_ref, g_ref, b_ref, ys_ref, x2_ref, xb_ref, ybuf, sem, *, n, alpha):
    i = pl.program_id(0)

    def row_copy(r, k):
        p = pos_ref[k * n + i * ROW_TD + r]
        return pltpu.make_async_copy(ys_ref.at[pl.ds(p, 1)], ybuf.at[k, pl.ds(r, 1)], sem)

    def issue(r, carry):
        row_copy(r, 0).start()
        row_copy(r, 1).start()
        return carry

    def drain(r, carry):
        row_copy(r, 0).wait()
        row_copy(r, 1).wait()
        return carry

    lax.fori_loop(0, ROW_TD, issue, 0)
    lax.fori_loop(0, ROW_TD, drain, 0)
    wt = wt_ref[...]
    ffn = wt[:, 0:1] * ybuf[0] + wt[:, 1:2] * ybuf[1]
    x2 = _layer_norm(alpha * x_ref[...] + ffn, g_ref[...], b_ref[...])
    x2_ref[...] = x2
    xb_ref[...] = x2.astype(xb_ref.dtype)


def _combine(pos, x1, wt, g, b, ys, alpha):
    n, d = x1.shape
    row = lambda width: pl.BlockSpec((ROW_TD, width), lambda i, p: (i, 0))
    vec = pl.BlockSpec((1, d), lambda i, p: (0, 0))
    return pl.pallas_call(
        functools.partial(_combine_kernel, n=n, alpha=alpha),
        grid_spec=pltpu.PrefetchScalarGridSpec(
            num_scalar_prefetch=1, grid=(n // ROW_TD,),
            in_specs=[row(d), row(LANES), vec, vec, pl.BlockSpec(memory_space=pl.ANY)],
            out_specs=[row(d), row(d)],
            scratch_shapes=[pltpu.VMEM((2, ROW_TD, d), F32), pltpu.SemaphoreType.DMA]),
        out_shape=[jax.ShapeDtypeStruct((n, d), F32), jax.ShapeDtypeStruct((n, d), BF16)],
        compiler_params=_cparams("arbitrary"),
        name="moe_combine_ln",
    )(pos, x1, wt, g, b, ys)


def _sorted_positions(idx, m_pad):
    e = idx[0:2].reshape(-1)
    onehot = (e[:, None] == jnp.arange(N_EXPERTS, dtype=jnp.int32)[None, :]).astype(jnp.int32)
    csum = jnp.cumsum(onehot, axis=0)
    rank = jnp.sum(onehot * csum, axis=1) - 1
    counts = csum[-1]
    padded = ((counts + MOE_TM - 1) // MOE_TM) * MOE_TM
    gend = jnp.cumsum(padded)
    gstart = gend - padded
    pos = jnp.sum(onehot * gstart[None, :], axis=1) + rank
    n_tiles = m_pad // MOE_TM
    n_used = gend[-1] // MOE_TM
    tile_row = jnp.minimum(jnp.arange(n_tiles, dtype=jnp.int32), n_used - 1) * MOE_TM
    tile_e = jnp.sum((tile_row[:, None] >= gend[None, :]).astype(jnp.int32), axis=1)
    return pos.astype(jnp.int32), tile_e.astype(jnp.int32), n_used.reshape(1).astype(jnp.int32)


def kernel(x, w_in, lam_q1, lam_k1, lam_q2, lam_k2, subln_g, conv_dw, conv_ln_g, conv_ln_b, conv_pw, rg_conv_w, rg_conv_b, rg_wa, rg_ba, rg_wx, rg_bx, rg_lambda, w_out, ln1_g, ln1_b, router_w, router_b, exp_wg, exp_wu, exp_wd, ln2_g, ln2_b):
    batch, seq, d = x.shape
    depth = w_in.shape[0]
    n = batch * seq
    att_w = ATT_HEADS * 2 * HEAD_DIM
    conv_w = conv_pw.shape[1]
    rnn_w = rg_conv_w.shape[2]
    qkv_w = 3 * att_w
    alpha = (2 * depth) ** 0.25
    m_pad = 2 * n + N_EXPERTS * MOE_TM
    assert w_in.shape[2] == qkv_w + 2 * conv_w + 2 * rnn_w and conv_w == rnn_w

    pos = jnp.arange(seq, dtype=F32)
    inv_freq = ROPE_THETA ** (-jnp.arange(0, HEAD_DIM, 2, dtype=F32) / HEAD_DIM)
    ang = pos[:, None] * inv_freq[None, :]
    cos2 = jnp.concatenate([jnp.cos(ang), jnp.cos(ang)], axis=1)
    sin2 = jnp.concatenate([-jnp.sin(ang), jnp.sin(ang)], axis=1)

    rw_pad = jnp.pad(router_w, ((0, 0), (0, LANES - N_EXPERTS))).astype(BF16)
    rb_col = router_b.astype(F32).reshape(N_EXPERTS, 1)

    xf = x.reshape(n, d).astype(F32)
    xb = xf.astype(BF16)
    for l in range(depth):
        w_in_b = w_in[l].astype(BF16)
        qkv = _inproj_qkv(xb, w_in_b[:, 0:qkv_w], cos2, sin2, seq)
        cr = _inproj_plain(xb, w_in_b[:, qkv_w:])
        lamv = jnp.stack([lam_q1[l], lam_k1[l], lam_q2[l], lam_k2[l]]).astype(F32)
        att = _attention(qkv, lamv, subln_g[l].reshape(1, -1), l, batch, seq)
        cv = _conformer_conv(cr, conv_dw[l], conv_ln_g[l].reshape(1, -1), conv_ln_b[l].reshape(1, -1),
                             conv_pw[l].astype(BF16), batch, seq)
        rn = _rglru(cr, rg_conv_w[l], rg_conv_b[l].reshape(1, -1), rg_wa[l].astype(BF16),
                    rg_ba[l].reshape(1, -1), rg_wx[l].astype(BF16), rg_bx[l].reshape(1, -1),
                    rg_lambda[l].reshape(1, -1), batch, seq, x_blk=2 * conv_w // rnn_w, gate_blk=2 * conv_w // rnn_w + 1)
        x1, idx, wt = _outproj_route(att, cv, rn, w_out[l].astype(BF16), xf, ln1_g[l].reshape(1, -1),
                                     ln1_b[l].reshape(1, -1), rw_pad, rb_col, alpha)
        slot, tile_e, n_used = _sorted_positions(idx, m_pad)
        xs = _dispatch(slot, x1, m_pad)
        ys = _moe(tile_e, n_used, xs, exp_wg[l].astype(BF16), exp_wu[l].astype(BF16), exp_wd[l].astype(BF16))
        xf, xb = _combine(slot, x1, wt, ln2_g[l].reshape(1, -1), ln2_b[l].reshape(1, -1), ys, alpha)
    return xf.reshape(batch, seq, d)
```

```python
import functools
import math

import jax
import jax.numpy as jnp
from jax import lax
from jax.experimental import pallas as pl
from jax.experimental.pallas import tpu as pltpu

F32 = jnp.float32
BF16 = jnp.bfloat16

ATT_HEADS = 4
HEAD_DIM = 128
ROPE_THETA = 10000.0
CONV_KERNEL = 31
RNN_CONV = 4
RNN_BLOCKS = 4
RG_C = 8.0
N_EXPERTS = 16
N_GROUPS = 4
EXPERTS_PER_GROUP = N_EXPERTS // N_GROUPS
EPS = 1e-5

LANES = 128
SUBLANES = 8
VMEM_LIMIT = 52 * 1024 * 1024

PROJ_TM = 1024
PROJ_TN = 1024
ATT_TQ = 256
CONV_RC = 64
CONV_PAD = 32
RNN_RC = 256
RNN_PAD = 8
OUT_TM = 256
MOE_TM = 256
MOE_VMEM_LIMIT = 58 * 1024 * 1024
CAST_ROWS = 256
DISP_TD = 1024
COMB_TD = 256


def _cparams(*sem):
    return pltpu.CompilerParams(dimension_semantics=sem, vmem_limit_bytes=VMEM_LIMIT)


def _layer_norm(z, g, b):
    mu = jnp.mean(z, axis=-1, keepdims=True)
    zc = z - mu
    var = jnp.mean(zc * zc, axis=-1, keepdims=True)
    return zc * lax.rsqrt(var + EPS) * g + b


def _row_windows(win, first, taps, rows):
    total = win.shape[0]
    rolled = {}
    out = []
    for j in range(taps):
        shift = (first + j) % SUBLANES
        if shift not in rolled:
            rolled[shift] = pltpu.roll(win, total - shift, axis=0) if shift else win
        base = first + j - shift
        out.append(rolled[shift][base:base + rows])
    return out


def _inproj_kernel(x_ref, w_ref, cos_ref, sin_ref, o_ref, wb_ref, *, scale, rope_tiles):
    j = pl.program_id(0)

    @pl.when(pl.program_id(1) == 0)
    def _():
        wb_ref[...] = w_ref[0].astype(BF16)

    acc = jnp.dot(x_ref[...], wb_ref[...], preferred_element_type=F32)
    if rope_tiles == 0:
        o_ref[...] = acc.astype(o_ref.dtype)
        return

    @pl.when(j < rope_tiles)
    def _():
        mult = jnp.where(j == 0, scale, 1.0).astype(F32)
        cos = cos_ref[...] * mult
        sin = sin_ref[...] * mult
        for c in range(PROJ_TN // LANES):
            t = acc[:, c * LANES:(c + 1) * LANES]
            r = t * cos + pltpu.roll(t, HEAD_DIM // 2, axis=1) * sin
            o_ref[:, c * LANES:(c + 1) * LANES] = r.astype(o_ref.dtype)

    @pl.when(j >= rope_tiles)
    def _():
        o_ref[...] = acc.astype(o_ref.dtype)


def _inproj(xb, w_in, layer, col0, width, cos2, sin2, seq, out_dtype, rope_tiles, name):
    n, d = xb.shape
    pos_blocks = seq // PROJ_TM
    tile0 = col0 // PROJ_TN
    return pl.pallas_call(
        functools.partial(_inproj_kernel, scale=HEAD_DIM ** -0.5, rope_tiles=rope_tiles),
        grid=(width // PROJ_TN, n // PROJ_TM),
        in_specs=[
            pl.BlockSpec((PROJ_TM, d), lambda j, i: (i, 0)),
            pl.BlockSpec((1, d, PROJ_TN), lambda j, i: (layer, 0, tile0 + j)),
            pl.BlockSpec((PROJ_TM, LANES), lambda j, i: (i % pos_blocks, 0)),
            pl.BlockSpec((PROJ_TM, LANES), lambda j, i: (i % pos_blocks, 0)),
        ],
        out_specs=pl.BlockSpec((PROJ_TM, PROJ_TN), lambda j, i: (i, j)),
        out_shape=jax.ShapeDtypeStruct((n, width), out_dtype),
        scratch_shapes=[pltpu.VMEM((d, PROJ_TN), BF16)],
        compiler_params=_cparams("arbitrary", "arbitrary"),
        name=name,
    )(xb, w_in, cos2, sin2)


def _attn_kernel(q_ref, k_ref, v_ref, lamv_ref, g_ref, o_ref, *, lam_init, seq):
    lamv = lamv_ref[...]
    lam = (jnp.exp(jnp.sum(lamv[0:1, :] * lamv[1:2, :], keepdims=True))
           - jnp.exp(jnp.sum(lamv[2:3, :] * lamv[3:4, :], keepdims=True)) + lam_init)
    kt1 = k_ref[:, 0:HEAD_DIM].T
    kt2 = k_ref[:, HEAD_DIM:2 * HEAD_DIM].T
    gain = g_ref[...] * (1.0 - lam_init)
    for qi in range(seq // ATT_TQ):
        lo, hi = qi * ATT_TQ, (qi + 1) * ATT_TQ
        row = lo + lax.broadcasted_iota(jnp.int32, (ATT_TQ, hi), 0)
        col = lax.broadcasted_iota(jnp.int32, (ATT_TQ, hi), 1)
        keep = col <= row

        def softmax_terms(q, kt):
            s = jnp.dot(q, kt[:, 0:hi], preferred_element_type=F32)
            s = jnp.where(keep, s, -jnp.inf)
            e = jnp.exp(s - jnp.max(s, axis=-1, keepdims=True))
            return e, jnp.sum(e, axis=-1, keepdims=True)

        e1, l1 = softmax_terms(q_ref[lo:hi, 0:HEAD_DIM], kt1)
        e2, l2 = softmax_terms(q_ref[lo:hi, HEAD_DIM:2 * HEAD_DIM], kt2)
        a = e1 * (1.0 / l1) - e2 * (lam / l2)
        o = jnp.dot(a.astype(BF16), v_ref[0:hi, :], preferred_element_type=F32)
        o = o * lax.rsqrt(jnp.mean(o * o, axis=-1, keepdims=True) + EPS) * gain
        o_ref[lo:hi, :] = o.astype(o_ref.dtype)


def _attention(qkv, lamv, subln_g, layer, batch, seq):
    vw = 2 * HEAD_DIM
    lam_init = 0.8 - 0.6 * math.exp(-0.3 * layer)
    return pl.pallas_call(
        functools.partial(_attn_kernel, lam_init=lam_init, seq=seq),
        grid=(batch, ATT_HEADS),
        in_specs=[
            pl.BlockSpec((seq, vw), lambda b, h: (b, h)),
            pl.BlockSpec((seq, vw), lambda b, h: (b, ATT_HEADS + h)),
            pl.BlockSpec((seq, vw), lambda b, h: (b, 2 * ATT_HEADS + h)),
            pl.BlockSpec((4, HEAD_DIM), lambda b, h: (0, 0)),
            pl.BlockSpec((1, vw), lambda b, h: (0, 0)),
        ],
        out_specs=pl.BlockSpec((seq, vw), lambda b, h: (b, h)),
        out_shape=jax.ShapeDtypeStruct((batch * seq, ATT_HEADS * vw), BF16),
        compiler_params=_cparams("arbitrary", "arbitrary"),
        name="diff_attention",
    )(qkv, qkv, qkv, lamv, subln_g)


def _conv_kernel(u_ref, dw_ref, g_ref, b_ref, pw_ref, o_ref, gpad_ref, y_ref, *, seq, width):
    gpad_ref[0:CONV_PAD, :] = jnp.zeros((CONV_PAD, width), F32)

    def glu_step(i, carry):
        r0 = pl.multiple_of(i * CONV_RC, CONV_RC)
        u = u_ref[pl.ds(r0, CONV_RC), :]
        gpad_ref[pl.ds(CONV_PAD + r0, CONV_RC), :] = u[:, 0:width] * jax.nn.sigmoid(u[:, width:2 * width])
        return carry

    lax.fori_loop(0, seq // CONV_RC, glu_step, 0)

    ln_g = g_ref[...]
    ln_b = b_ref[...]
    first = CONV_PAD - (CONV_KERNEL - 1)

    def conv_step(i, carry):
        r0 = pl.multiple_of(i * CONV_RC, CONV_RC)
        win = gpad_ref[pl.ds(r0, CONV_RC + CONV_PAD), :]
        acc = jnp.zeros((CONV_RC, width), F32)
        for j, rows in enumerate(_row_windows(win, first, CONV_KERNEL, CONV_RC)):
            acc = acc + dw_ref[j:j + 1, :] * rows
        y = _layer_norm(acc, ln_g, ln_b)
        y = y * jax.nn.sigmoid(y)
        y_ref[pl.ds(r0, CONV_RC), :] = y.astype(y_ref.dtype)
        return carry

    lax.fori_loop(0, seq // CONV_RC, conv_step, 0)
    o_ref[...] = jnp.dot(y_ref[...], pw_ref[...], preferred_element_type=F32).astype(o_ref.dtype)


def _conformer_conv(cr, dw, ln_g, ln_b, pw, batch, seq):
    width = pw.shape[0]
    return pl.pallas_call(
        functools.partial(_conv_kernel, seq=seq, width=width),
        grid=(batch,),
        in_specs=[
            pl.BlockSpec((seq, 2 * width), lambda b: (b, 0)),
            pl.BlockSpec((CONV_KERNEL, width), lambda b: (0, 0)),
            pl.BlockSpec((1, width), lambda b: (0, 0)),
            pl.BlockSpec((1, width), lambda b: (0, 0)),
            pl.BlockSpec((width, width), lambda b: (0, 0)),
        ],
        out_specs=pl.BlockSpec((seq, width), lambda b: (b, 0)),
        out_shape=jax.ShapeDtypeStruct((batch * seq, width), BF16),
        scratch_shapes=[pltpu.VMEM((CONV_PAD + seq, width), F32), pltpu.VMEM((seq, width), BF16)],
        compiler_params=_cparams("arbitrary"),
        name="conformer_conv",
    )(cr, dw, ln_g, ln_b, pw)


def _rglru_kernel(x_ref, gate_ref, cw_ref, cb_ref, wa_ref, ba_ref, wx_ref, bx_ref, lam_ref, o_ref,
                  xpad_ref, a_ref, b_ref, *, seq, width):
    bd = width // RNN_BLOCKS
    xpad_ref[0:RNN_PAD, :] = jnp.zeros((RNN_PAD, width), F32)
    xpad_ref[RNN_PAD:RNN_PAD + seq, :] = x_ref[...]

    lam = lam_ref[...]
    neg = -lam
    softplus = jnp.maximum(neg, 0.0) + jnp.log1p(jnp.exp(-jnp.abs(neg)))
    decay = -RG_C * softplus
    cb = cb_ref[...]
    ba = ba_ref[...]
    bx = bx_ref[...]
    first = RNN_PAD - (RNN_CONV - 1)

    def gate_step(i, carry):
        r0 = pl.multiple_of(i * RNN_RC, RNN_RC)
        win = xpad_ref[pl.ds(r0, RNN_RC + RNN_PAD), :]
        xc = jnp.zeros((RNN_RC, width), F32) + cb
        for j, rows in enumerate(_row_windows(win, first, RNN_CONV, RNN_RC)):
            xc = xc + cw_ref[j:j + 1, :] * rows
        xcb = xc.astype(BF16)
        ra = jnp.concatenate(
            [jnp.dot(xcb[:, n * bd:(n + 1) * bd], wa_ref[n], preferred_element_type=F32)
             for n in range(RNN_BLOCKS)], axis=1)
        rx = jnp.concatenate(
            [jnp.dot(xcb[:, n * bd:(n + 1) * bd], wx_ref[n], preferred_element_type=F32)
             for n in range(RNN_BLOCKS)], axis=1)
        r = jax.nn.sigmoid(ra + ba)
        ig = jax.nn.sigmoid(rx + bx)
        log_a = decay * r
        a = jnp.exp(log_a)
        a_ref[pl.ds(r0, RNN_RC), :] = a
        b_ref[pl.ds(r0, RNN_RC), :] = jnp.sqrt(-jnp.tanh(log_a) * (a * a + 1.0)) * (ig * xc)
        return carry

    lax.fori_loop(0, seq // RNN_RC, gate_step, 0)

    rows = lax.broadcasted_iota(jnp.int32, (SUBLANES, width), 0)

    def scan_step(c, h_prev):
        r0 = pl.multiple_of(c * SUBLANES, SUBLANES)
        a = a_ref[pl.ds(r0, SUBLANES), :]
        b = b_ref[pl.ds(r0, SUBLANES), :]
        for sh in (1, 2, 4):
            valid = rows >= sh
            b = jnp.where(valid, a * pltpu.roll(b, sh, axis=0) + b, b)
            a = jnp.where(valid, a * pltpu.roll(a, sh, axis=0), a)
        h = a * h_prev + b
        b_ref[pl.ds(r0, SUBLANES), :] = h
        return jnp.broadcast_to(h[SUBLANES - 1:SUBLANES, :], (SUBLANES, width))

    lax.fori_loop(0, seq // SUBLANES, scan_step, jnp.zeros((SUBLANES, width), F32))

    def out_step(i, carry):
        r0 = pl.multiple_of(i * RNN_RC, RNN_RC)
        g = gate_ref[pl.ds(r0, RNN_RC), :]
        cdf = 0.5 * (1.0 + jnp.tanh(math.sqrt(2.0 / math.pi) * (g + 0.044715 * (g * g * g))))
        o_ref[pl.ds(r0, RNN_RC), :] = (b_ref[pl.ds(r0, RNN_RC), :] * (g * cdf)).astype(o_ref.dtype)
        return carry

    lax.fori_loop(0, seq // RNN_RC, out_step, 0)


def _rglru(cr, cw, cb, wa, ba, wx, bx, lam, batch, seq, x_blk, gate_blk):
    width = cw.shape[1]
    bd = width // RNN_BLOCKS
    vec = pl.BlockSpec((1, width), lambda b: (0, 0))
    blk = pl.BlockSpec((RNN_BLOCKS, bd, bd), lambda b: (0, 0, 0))
    return pl.pallas_call(
        functools.partial(_rglru_kernel, seq=seq, width=width),
        grid=(batch,),
        in_specs=[
            pl.BlockSpec((seq, width), lambda b: (b, x_blk)),
            pl.BlockSpec((seq, width), lambda b: (b, gate_blk)),
            pl.BlockSpec((RNN_CONV, width), lambda b: (0, 0)),
            vec, blk, vec, blk, vec, vec,
        ],
        out_specs=pl.BlockSpec((seq, width), lambda b: (b, 0)),
        out_shape=jax.ShapeDtypeStruct((batch * seq, width), BF16),
        scratch_shapes=[pltpu.VMEM((RNN_PAD + seq, width), F32), pltpu.VMEM((seq, width), F32),
                        pltpu.VMEM((seq, width), F32)],
        compiler_params=_cparams("arbitrary"),
        name="rglru",
    )(cr, cr, cw, cb, wa, ba, wx, bx, lam)


def _route(scores, biased):
    neg_inf = jnp.float32(-jnp.inf)
    group_score = []
    for g in range(N_GROUPS):
        v = biased[g * EXPERTS_PER_GROUP:(g + 1) * EXPERTS_PER_GROUP]
        best = None
        for i in range(EXPERTS_PER_GROUP):
            for j in range(i + 1, EXPERTS_PER_GROUP):
                pair = v[i] + v[j]
                best = pair if best is None else jnp.maximum(best, pair)
        group_score.append(best)
    sel = jnp.zeros_like(group_score[0], dtype=jnp.int32)
    top = group_score[0]
    for g in range(1, N_GROUPS):
        better = group_score[g] > top
        sel = jnp.where(better, g, sel)
        top = jnp.where(better, group_score[g], top)
    masked = [jnp.where(sel == (e // EXPERTS_PER_GROUP), biased[e], neg_inf) for e in range(N_EXPERTS)]

    def first_argmax(vals):
        m = vals[0]
        for e in range(1, N_EXPERTS):
            m = jnp.maximum(m, vals[e])
        idx = jnp.full_like(sel, N_EXPERTS)
        for e in range(N_EXPERTS - 1, -1, -1):
            idx = jnp.where(vals[e] == m, e, idx)
        return idx

    idx1 = first_argmax(masked)
    idx2 = first_argmax([jnp.where(idx1 == e, neg_inf, masked[e]) for e in range(N_EXPERTS)])
    s1 = jnp.zeros_like(top)
    s2 = jnp.zeros_like(top)
    for e in range(N_EXPERTS):
        s1 = jnp.where(idx1 == e, scores[e], s1)
        s2 = jnp.where(idx2 == e, scores[e], s2)
    tot = s1 + s2
    return idx1, idx2, s1 / tot, s2 / tot


def _outproj_kernel(att_ref, cv_ref, rn_ref, w_ref, x_ref, g_ref, b_ref, rw_ref, rb_ref,
                    x1_ref, idx_ref, wt_ref, wb_ref, *, alpha):
    @pl.when(pl.program_id(0) == 0)
    def _():
        wb_ref[...] = w_ref[0].astype(BF16)

    cat = jnp.concatenate([att_ref[...], cv_ref[...], rn_ref[...]], axis=1)
    mix = jnp.dot(cat, wb_ref[...], preferred_element_type=F32)
    x1 = _layer_norm(alpha * x_ref[...] + mix, g_ref[...], b_ref[...])
    x1_ref[...] = x1
    logits = jnp.dot(x1.astype(BF16), rw_ref[...], preferred_element_type=F32)
    lt = logits.T
    sc = jax.nn.sigmoid(lt[0:N_EXPERTS, :])
    bs = sc + rb_ref[...]
    scores = [sc[e:e + 1, :] for e in range(N_EXPERTS)]
    biased = [bs[e:e + 1, :] for e in range(N_EXPERTS)]
    idx1, idx2, w1, w2 = _route(scores, biased)
    t = x1.shape[0]
    idx_ref[...] = jnp.concatenate([idx1, idx2, jnp.zeros((SUBLANES - 2, t), jnp.int32)], axis=0)
    wt_ref[...] = jnp.concatenate([w1, w2, jnp.zeros((LANES - 2, t), F32)], axis=0).T


def _outproj_route(att, cv, rn, w_out, layer, x, g, b, rw_pad, rb_col, alpha):
    n, d = x.shape
    d_mix = w_out.shape[1]
    row = lambda width: pl.BlockSpec((OUT_TM, width), lambda i: (i, 0))
    full = lambda shape: pl.BlockSpec(shape, lambda i: (0,) * len(shape))
    return pl.pallas_call(
        functools.partial(_outproj_kernel, alpha=alpha),
        grid=(n // OUT_TM,),
        in_specs=[row(att.shape[1]), row(cv.shape[1]), row(rn.shape[1]),
                  pl.BlockSpec((1, d_mix, d), lambda i: (layer, 0, 0), pipeline_mode=pl.Buffered(1)),
                  row(d), full((1, d)), full((1, d)), full(rw_pad.shape), full(rb_col.shape)],
        out_specs=[row(d), pl.BlockSpec((SUBLANES, OUT_TM), lambda i: (0, i)), row(LANES)],
        out_shape=[jax.ShapeDtypeStruct((n, d), F32), jax.ShapeDtypeStruct((SUBLANES, n), jnp.int32),
                   jax.ShapeDtypeStruct((n, LANES), F32)],
        scratch_shapes=[pltpu.VMEM((d_mix, d), BF16)],
        compiler_params=_cparams("arbitrary"),
        name="outproj_ln_router",
    )(att, cv, rn, w_out, x, g, b, rw_pad, rb_col)


def _dispatch_kernel(pos_ref, pad0_ref, padn_ref, nu_ref, x_ref, xs_ref, zero_ref, sem, zsem, *, n):
    i = pl.program_id(0)
    n_tiles = xs_ref.shape[0] // MOE_TM

    def pad_copy(e, r):
        return pltpu.make_async_copy(zero_ref.at[pl.ds(0, 1)], xs_ref.at[pl.ds(pad0_ref[e] + r, 1)], zsem)

    def tile_copy(t):
        return pltpu.make_async_copy(zero_ref, xs_ref.at[pl.ds(pl.multiple_of(t * MOE_TM, MOE_TM), MOE_TM)], zsem)

    def for_each_pad_row(fn):
        for e in range(N_EXPERTS):
            def body(r, carry, e=e):
                fn(pad_copy(e, r))
                return carry
            lax.fori_loop(0, padn_ref[e], body, 0)

        def tile_body(t, carry):
            fn(tile_copy(t))
            return carry
        lax.fori_loop(nu_ref[0], n_tiles, tile_body, 0)

    @pl.when(i == 0)
    def _():
        zero_ref[...] = jnp.zeros_like(zero_ref)
        for_each_pad_row(lambda c: c.start())

    def issue(grp, carry):
        r0 = pl.multiple_of(grp * SUBLANES, SUBLANES)
        for j in range(SUBLANES):
            for k in range(2):
                p = pos_ref[k * n + i * DISP_TD + r0 + j]
                pltpu.make_async_copy(x_ref.at[pl.ds(r0 + j, 1)], xs_ref.at[pl.ds(p, 1)], sem).start()
        return carry

    lax.fori_loop(0, DISP_TD // SUBLANES, issue, 0)
    for k in range(2):
        pltpu.make_async_copy(x_ref, xs_ref.at[pl.ds(0, DISP_TD)], sem).wait()

    @pl.when(i == 0)
    def _():
        for_each_pad_row(lambda c: c.wait())


def _dispatch(pos, pad0, padn, n_used, x1, m_pad):
    n, d = x1.shape
    return pl.pallas_call(
        functools.partial(_dispatch_kernel, n=n),
        grid_spec=pltpu.PrefetchScalarGridSpec(
            num_scalar_prefetch=4, grid=(n // DISP_TD,),
            in_specs=[pl.BlockSpec((DISP_TD, d), lambda i, *_: (i, 0))],
            out_specs=pl.BlockSpec(memory_space=pl.ANY),
            scratch_shapes=[pltpu.VMEM((MOE_TM, d), F32), pltpu.SemaphoreType.DMA, pltpu.SemaphoreType.DMA]),
        out_shape=jax.ShapeDtypeStruct((m_pad, d), F32),
        compiler_params=_cparams("arbitrary"),
        name="moe_dispatch",
    )(pos, pad0, padn, n_used, x1)


def _moe_kernel(te_ref, first_ref, nxt_ref, nu_ref, xs_ref, wg_hbm, wu_hbm, wd_hbm, ys_ref,
                sg_ref, su_ref, sd_ref, bg_ref, bu_ref, bd_ref, sem, *, layer):
    t = pl.program_id(0)
    used = t < nu_ref[0]

    def weight_copies(e):
        return (pltpu.make_async_copy(wg_hbm.at[layer, e], sg_ref, sem.at[0]),
                pltpu.make_async_copy(wu_hbm.at[layer, e], su_ref, sem.at[1]),
                pltpu.make_async_copy(wd_hbm.at[layer, e], sd_ref, sem.at[2]))

    def round_to_bf16(src_ref, dst_ref):
        def body(c, carry):
            r0 = pl.multiple_of(c * CAST_ROWS, CAST_ROWS)
            dst_ref[pl.ds(r0, CAST_ROWS), :] = src_ref[pl.ds(r0, CAST_ROWS), :].astype(BF16)
            return carry
        lax.fori_loop(0, src_ref.shape[0] // CAST_ROWS, body, 0)

    @pl.when(t == 0)
    def _():
        for c in weight_copies(te_ref[0]):
            c.start()

    @pl.when(jnp.logical_and(used, first_ref[t] == 1))
    def _():
        for c in weight_copies(te_ref[t]):
            c.wait()
        round_to_bf16(sg_ref, bg_ref)
        round_to_bf16(su_ref, bu_ref)
        round_to_bf16(sd_ref, bd_ref)

        @pl.when(nxt_ref[t] >= 0)
        def _():
            for c in weight_copies(nxt_ref[t]):
                c.start()

    @pl.when(used)
    def _():
        x = xs_ref[...].astype(BF16)
        g = jnp.dot(x, bg_ref[...], preferred_element_type=F32)
        u = jnp.dot(x, bu_ref[...], preferred_element_type=F32)
        h = (g * jax.nn.sigmoid(g)) * u
        ys_ref[...] = jnp.dot(h.astype(BF16), bd_ref[...], preferred_element_type=F32)

    @pl.when(jnp.logical_not(used))
    def _():
        ys_ref[...] = jnp.zeros_like(ys_ref)


def _moe(tile_e, first, nxt, n_used, xs, wg, wu, wd, layer):
    m_pad, d = xs.shape
    de = wg.shape[3]
    return pl.pallas_call(
        functools.partial(_moe_kernel, layer=layer),
        grid_spec=pltpu.PrefetchScalarGridSpec(
            num_scalar_prefetch=4, grid=(m_pad // MOE_TM,),
            in_specs=[
                pl.BlockSpec((MOE_TM, d), lambda t, *_: (t, 0)),
                pl.BlockSpec(memory_space=pl.ANY),
                pl.BlockSpec(memory_space=pl.ANY),
                pl.BlockSpec(memory_space=pl.ANY),
            ],
            out_specs=pl.BlockSpec((MOE_TM, d), lambda t, *_: (t, 0)),
            scratch_shapes=[pltpu.VMEM((d, de), F32), pltpu.VMEM((d, de), F32), pltpu.VMEM((de, d), F32),
                            pltpu.VMEM((d, de), BF16), pltpu.VMEM((d, de), BF16), pltpu.VMEM((de, d), BF16),
                            pltpu.SemaphoreType.DMA((3,))]),
        out_shape=jax.ShapeDtypeStruct((m_pad, d), F32),
        compiler_params=pltpu.CompilerParams(dimension_semantics=("arbitrary",), vmem_limit_bytes=MOE_VMEM_LIMIT),
        name="moe_experts",
    )(tile_e, first, nxt, n_used, xs, wg, wu, wd)


def _combine_kernel(pos_ref, x_ref, wt_ref, g_ref, b_ref, ys_ref, x2_ref, xb_ref, ybuf, sem, *, n, alpha):
    i = pl.program_id(0)
    slot = i % 2

    def issue(block, into):
        def body(grp, carry):
            r0 = pl.multiple_of(grp * SUBLANES, SUBLANES)
            for j in range(SUBLANES):
                for k in range(2):
                    p = pos_ref[k * n + block * COMB_TD + r0 + j]
                    pltpu.make_async_copy(ys_ref.at[pl.ds(p, 1)], ybuf.at[into, k, pl.ds(r0 + j, 1)],
                                          sem.at[into]).start()
            return carry
        lax.fori_loop(0, COMB_TD // SUBLANES, body, 0)

    @pl.when(i == 0)
    def _():
        issue(0, 0)

    @pl.when(i + 1 < pl.num_programs(0))
    def _():
        issue(i + 1, 1 - slot)

    for k in range(2):
        pltpu.make_async_copy(ys_ref.at[pl.ds(0, COMB_TD)], ybuf.at[slot, k], sem.at[slot]).wait()
    wt = wt_ref[...]
    ffn = wt[:, 0:1] * ybuf[slot, 0] + wt[:, 1:2] * ybuf[slot, 1]
    x2 = _layer_norm(alpha * x_ref[...] + ffn, g_ref[...], b_ref[...])
    x2_ref[...] = x2
    xb_ref[...] = x2.astype(xb_ref.dtype)


def _combine(pos, x1, wt, g, b, ys, alpha):
    n, d = x1.shape
    row = lambda width: pl.BlockSpec((COMB_TD, width), lambda i, p: (i, 0))
    vec = pl.BlockSpec((1, d), lambda i, p: (0, 0))
    return pl.pallas_call(
        functools.partial(_combine_kernel, n=n, alpha=alpha),
        grid_spec=pltpu.PrefetchScalarGridSpec(
            num_scalar_prefetch=1, grid=(n // COMB_TD,),
            in_specs=[row(d), row(LANES), vec, vec, pl.BlockSpec(memory_space=pl.ANY)],
            out_specs=[row(d), row(d)],
            scratch_shapes=[pltpu.VMEM((2, 2, COMB_TD, d), F32), pltpu.SemaphoreType.DMA((2,))]),
        out_shape=[jax.ShapeDtypeStruct((n, d), F32), jax.ShapeDtypeStruct((n, d), BF16)],
        compiler_params=_cparams("arbitrary"),
        name="moe_combine_ln",
    )(pos, x1, wt, g, b, ys)


def _sorted_positions(idx, m_pad):
    e = idx[0:2].reshape(-1)
    onehot = (e[:, None] == jnp.arange(N_EXPERTS, dtype=jnp.int32)[None, :]).astype(jnp.int32)
    csum = jnp.cumsum(onehot, axis=0)
    rank = jnp.sum(onehot * csum, axis=1) - 1
    counts = csum[-1]
    padded = ((counts + MOE_TM - 1) // MOE_TM) * MOE_TM
    gend = jnp.cumsum(padded)
    gstart = gend - padded
    pos = jnp.sum(onehot * gstart[None, :], axis=1) + rank
    n_tiles = m_pad // MOE_TM
    n_used = gend[-1] // MOE_TM
    tile_id = jnp.arange(n_tiles, dtype=jnp.int32)
    tile_row = jnp.minimum(tile_id, n_used - 1) * MOE_TM
    tile_e = jnp.sum((tile_row[:, None] >= gend[None, :]).astype(jnp.int32), axis=1)
    first = jnp.logical_and(tile_row == gstart[tile_e], tile_id < n_used)
    ids = jnp.arange(N_EXPERTS, dtype=jnp.int32)
    later_used = jnp.logical_and(counts[None, :] > 0, ids[None, :] > ids[:, None])
    nxt_e = jnp.min(jnp.where(later_used, ids[None, :], N_EXPERTS), axis=1)
    nxt_e = jnp.where(nxt_e == N_EXPERTS, -1, nxt_e)
    i32 = lambda a: a.astype(jnp.int32)
    return dict(pos=i32(pos), tile_e=i32(tile_e), first=i32(first), nxt=i32(nxt_e[tile_e]),
                n_used=i32(n_used.reshape(1)), pad0=i32(gstart + counts), padn=i32(padded - counts))


def kernel(x, w_in, lam_q1, lam_k1, lam_q2, lam_k2, subln_g, conv_dw, conv_ln_g, conv_ln_b, conv_pw, rg_conv_w, rg_conv_b, rg_wa, rg_ba, rg_wx, rg_bx, rg_lambda, w_out, ln1_g, ln1_b, router_w, router_b, exp_wg, exp_wu, exp_wd, ln2_g, ln2_b):
    batch, seq, d = x.shape
    depth = w_in.shape[0]
    n = batch * seq
    att_w = ATT_HEADS * 2 * HEAD_DIM
    conv_w = conv_pw.shape[1]
    rnn_w = rg_conv_w.shape[2]
    qkv_w = 3 * att_w
    alpha = (2 * depth) ** 0.25
    m_pad = 2 * n + N_EXPERTS * MOE_TM
    assert w_in.shape[2] == qkv_w + 2 * conv_w + 2 * rnn_w and conv_w == rnn_w

    pos = jnp.arange(seq, dtype=F32)
    inv_freq = ROPE_THETA ** (-jnp.arange(0, HEAD_DIM, 2, dtype=F32) / HEAD_DIM)
    ang = pos[:, None] * inv_freq[None, :]
    cos2 = jnp.concatenate([jnp.cos(ang), jnp.cos(ang)], axis=1)
    sin2 = jnp.concatenate([-jnp.sin(ang), jnp.sin(ang)], axis=1)

    rw_pad = jnp.pad(router_w, ((0, 0), (0, LANES - N_EXPERTS))).astype(BF16)
    rb_col = router_b.astype(F32).reshape(N_EXPERTS, 1)

    xf = x.reshape(n, d).astype(F32)
    xb = xf.astype(BF16)
    for l in range(depth):
        qkv = _inproj(xb, w_in, l, 0, qkv_w, cos2, sin2, seq, BF16, 2, "inproj_qkv")
        cr = _inproj(xb, w_in, l, qkv_w, 2 * conv_w + 2 * rnn_w, cos2, sin2, seq, F32, 0, "inproj_cr")
        lamv = jnp.stack([lam_q1[l], lam_k1[l], lam_q2[l], lam_k2[l]]).astype(F32)
        att = _attention(qkv, lamv, subln_g[l].reshape(1, -1), l, batch, seq)
        cv = _conformer_conv(cr, conv_dw[l], conv_ln_g[l].reshape(1, -1), conv_ln_b[l].reshape(1, -1),
                             conv_pw[l].astype(BF16), batch, seq)
        rn = _rglru(cr, rg_conv_w[l], rg_conv_b[l].reshape(1, -1), rg_wa[l].astype(BF16),
                    rg_ba[l].reshape(1, -1), rg_wx[l].astype(BF16), rg_bx[l].reshape(1, -1),
                    rg_lambda[l].reshape(1, -1), batch, seq, x_blk=2 * conv_w // rnn_w, gate_blk=2 * conv_w // rnn_w + 1)
        x1, idx, wt = _outproj_route(att, cv, rn, w_out, l, xf, ln1_g[l].reshape(1, -1),
                                     ln1_b[l].reshape(1, -1), rw_pad, rb_col, alpha)
        s = _sorted_positions(idx, m_pad)
        xs = _dispatch(s["pos"], s["pad0"], s["padn"], s["n_used"], x1, m_pad)
        ys = _moe(s["tile_e"], s["first"], s["nxt"], s["n_used"], xs, exp_wg, exp_wu, exp_wd, l)
        xf, xb = _combine(s["pos"], x1, wt, ln2_g[l].reshape(1, -1), ln2_b[l].reshape(1, -1), ys, alpha)
    return xf.reshape(batch, seq, d)
```
